```python
import jax, jax.numpy as jnp
from jax import lax
import numpy as np

D_MODEL = 4096
BATCH = 2
SEQ = 8192
DEPTH = 2

CHUNK = 64
N_MIXERS = 2
N_HEADS = 32
HEAD_DIM = D_MODEL // N_HEADS
Q_BLOCK = 128
D_FF = 11008
CONV_WIDTH = 3
FFN_HALF = 0.5
NORM_EPS = 1e-6
N_ATTN = (DEPTH + 1) // 2
N_CONV = DEPTH // 2

kernel_name = "hybrid_stickbreak_shortconv_macaron"


def rmsnorm(x, g):
    xf = x.astype(jnp.float32)
    y = xf * lax.rsqrt(jnp.mean(xf * xf, axis=-1, keepdims=True) + NORM_EPS)
    return (y * g.astype(jnp.float32)).astype(x.dtype)


def swiglu(x, wi, wo):
    gate, up = jnp.split(x @ wi, 2, axis=-1)
    return (jax.nn.silu(gate) * up) @ wo


def stick_breaking_attention(x, wqkv, wo):
    b, s, d = x.shape
    q, k, v = jnp.split(x @ wqkv, 3, axis=-1)
    q = q.reshape(b, s, N_HEADS, HEAD_DIM)
    k = k.reshape(b, s, N_HEADS, HEAD_DIM)
    v = v.reshape(b, s, N_HEADS, HEAD_DIM)
    scale = HEAD_DIM ** -0.5
    outs = []
    for qb in range(s // Q_BLOCK):
        q0 = qb * Q_BLOCK
        kl = q0 + Q_BLOCK
        z = jnp.einsum('bqhd,bkhd->bhqk', q[:, q0:kl], k[:, :kl],
                       preferred_element_type=jnp.float32) * scale
        t_idx = q0 + jnp.arange(Q_BLOCK)
        s_idx = jnp.arange(kl)
        strict = s_idx[None, :] < t_idx[:, None]
        log_fail = jnp.where(strict, jax.nn.log_sigmoid(-z), 0.0)
        tail = lax.cumsum(log_fail, axis=3, reverse=True) - log_fail
        a = jnp.where(strict, jnp.exp(jax.nn.log_sigmoid(z) + tail), 0.0)
        outs.append(jnp.einsum('bhqk,bkhd->bqhd', a.astype(v.dtype), v[:, :kl]))
    o = jnp.concatenate(outs, axis=1).reshape(b, s, d)
    return o @ wo


def short_gated_conv(x, win, conv_w, wout):
    d = x.shape[-1]
    gb, gc, h = jnp.split(x @ win, 3, axis=-1)
    u = gc * h
    y = lax.conv_general_dilated(
        u, conv_w[:, None, :], window_strides=(1,),
        padding=[(CONV_WIDTH - 1, 0)],
        dimension_numbers=('NWC', 'WIO', 'NWC'),
        feature_group_count=d)
    return (gb * y) @ wout


def setup_inputs(seed: int = 0) -> dict:
    key = jax.random.key(seed)
    ks = jax.random.split(key, 11)
    f32 = jnp.float32
    x = jax.random.normal(ks[0], (BATCH, SEQ, D_MODEL), f32)
    ffn_norm = 1.0 + 0.02 * jax.random.normal(ks[1], (DEPTH, 2, D_MODEL), f32)
    ffn_wi = jax.random.normal(ks[2], (DEPTH, 2, D_MODEL, 2 * D_FF), f32) * D_MODEL ** -0.5
    ffn_wo = jax.random.normal(ks[3], (DEPTH, 2, D_FF, D_MODEL), f32) * D_FF ** -0.5
    mix_norm = 1.0 + 0.02 * jax.random.normal(ks[4], (DEPTH, D_MODEL), f32)
    attn_wqkv = jax.random.normal(ks[5], (N_ATTN, D_MODEL, 3 * D_MODEL), f32) * D_MODEL ** -0.5
    attn_wo = jax.random.normal(ks[6], (N_ATTN, D_MODEL, D_MODEL), f32) * D_MODEL ** -0.5
    conv_win = jax.random.normal(ks[7], (N_CONV, D_MODEL, 3 * D_MODEL), f32) * D_MODEL ** -0.5
    conv_w = jax.random.normal(ks[8], (N_CONV, CONV_WIDTH, D_MODEL), f32) * CONV_WIDTH ** -0.5
    conv_wout = jax.random.normal(ks[9], (N_CONV, D_MODEL, D_MODEL), f32) * D_MODEL ** -0.5
    final_norm = 1.0 + 0.02 * jax.random.normal(ks[10], (D_MODEL,), f32)
    return {"x": x, "ffn_norm": ffn_norm, "ffn_wi": ffn_wi, "ffn_wo": ffn_wo,
            "mix_norm": mix_norm, "attn_wqkv": attn_wqkv, "attn_wo": attn_wo,
            "conv_win": conv_win, "conv_w": conv_w, "conv_wout": conv_wout,
            "final_norm": final_norm}


def reference(x, ffn_norm, ffn_wi, ffn_wo, mix_norm, attn_wqkv, attn_wo,
              conv_win, conv_w, conv_wout, final_norm):
    for i in range(DEPTH):
        x = x + FFN_HALF * swiglu(rmsnorm(x, ffn_norm[i, 0]), ffn_wi[i, 0], ffn_wo[i, 0])
        h = rmsnorm(x, mix_norm[i])
        j = i // N_MIXERS
        if i % N_MIXERS == 0:
            x = x + stick_breaking_attention(h, attn_wqkv[j], attn_wo[j])
        else:
            x = x + short_gated_conv(h, conv_win[j], conv_w[j], conv_wout[j])
        x = x + FFN_HALF * swiglu(rmsnorm(x, ffn_norm[i, 1]), ffn_wi[i, 1], ffn_wo[i, 1])
    return rmsnorm(x, final_norm)
```

```python
import functools

import jax
import jax.numpy as jnp
from jax import lax
from jax.experimental import pallas as pl
from jax.experimental.pallas import tpu as pltpu

N_HEADS = 32
HEAD_DIM = 128
CONV_WIDTH = 3
FFN_HALF = 0.5
NORM_EPS = 1e-6

V7X_VMEM_BYTES = 64 * 1024 * 1024
VMEM_LIMIT_BYTES = 56 * 1024 * 1024
MXU_TILE = 256

F32 = jnp.float32
BF16 = jnp.bfloat16


def _params(semantics):
    return pltpu.CompilerParams(dimension_semantics=semantics, vmem_limit_bytes=VMEM_LIMIT_BYTES)


def _rmsnorm_kernel(x_ref, g_ref, o_ref):
    x = x_ref[...]
    ms = jnp.mean(x * x, axis=-1, keepdims=True)
    o_ref[...] = (x * lax.rsqrt(ms + NORM_EPS) * g_ref[...]).astype(o_ref.dtype)


def rmsnorm(x, g, out_dtype, tm=512):
    n, d = x.shape
    return pl.pallas_call(
        _rmsnorm_kernel,
        grid=(n // tm,),
        in_specs=[pl.BlockSpec((tm, d), lambda i: (i, 0)), pl.BlockSpec((1, d), lambda i: (0, 0))],
        out_specs=pl.BlockSpec((tm, d), lambda i: (i, 0)),
        out_shape=jax.ShapeDtypeStruct((n, d), out_dtype),
        compiler_params=_params(("parallel",)),
        name="rmsnorm",
    )(x, g.reshape(1, d))


def _mm_kernel(a_ref, w_ref, o_ref):
    o_ref[...] = jnp.dot(a_ref[...], w_ref[...], preferred_element_type=F32).astype(o_ref.dtype)


def matmul(a, w, out_dtype, tm=1024, tn=1024):
    n, k = a.shape
    m = w.shape[1]
    return pl.pallas_call(
        _mm_kernel,
        grid=(n // tm, m // tn),
        in_specs=[pl.BlockSpec((tm, k), lambda i, j: (i, 0)), pl.BlockSpec((k, tn), lambda i, j: (0, j))],
        out_specs=pl.BlockSpec((tm, tn), lambda i, j: (i, j)),
        out_shape=jax.ShapeDtypeStruct((n, m), out_dtype),
        compiler_params=_params(("parallel", "arbitrary")),
        name="matmul",
    )(a, w)


def _swiglu_kernel(h_ref, wg_ref, wu_ref, o_ref):
    h = h_ref[...]
    gate = jnp.dot(h, wg_ref[...], preferred_element_type=F32)
    up = jnp.dot(h, wu_ref[...], preferred_element_type=F32)
    o_ref[...] = (gate * jax.nn.sigmoid(gate) * up).astype(o_ref.dtype)


def matmul_swiglu(h, wg, wu, tm=1024, tf=512):
    n, k = h.shape
    f = wg.shape[1]
    return pl.pallas_call(
        _swiglu_kernel,
        grid=(n // tm, f // tf),
        in_specs=[
            pl.BlockSpec((tm, k), lambda i, j: (i, 0)),
            pl.BlockSpec((k, tf), lambda i, j: (0, j)),
            pl.BlockSpec((k, tf), lambda i, j: (0, j)),
        ],
        out_specs=pl.BlockSpec((tm, tf), lambda i, j: (i, j)),
        out_shape=jax.ShapeDtypeStruct((n, f), BF16),
        compiler_params=_params(("parallel", "arbitrary")),
        name="matmul_swiglu",
    )(h, wg, wu)


def _mm_residual_kernel(a_ref, w_ref, r_ref, o_ref, *, scale):
    kk = pl.program_id(2)
    part = scale * jnp.dot(a_ref[...], w_ref[...], preferred_element_type=F32)

    @pl.when(kk == 0)
    def _():
        o_ref[...] = r_ref[...] + part

    @pl.when(kk != 0)
    def _():
        o_ref[...] += part


def matmul_residual(a, w, res, scale, tm=1024, tn=2048, tk=1024):
    n, k = a.shape
    m = w.shape[1]
    return pl.pallas_call(
        functools.partial(_mm_residual_kernel, scale=scale),
        grid=(n // tm, m // tn, k // tk),
        in_specs=[
            pl.BlockSpec((tm, tk), lambda i, j, kk: (i, kk)),
            pl.BlockSpec((tk, tn), lambda i, j, kk: (kk, j)),
            pl.BlockSpec((tm, tn), lambda i, j, kk: (i, j)),
        ],
        out_specs=pl.BlockSpec((tm, tn), lambda i, j, kk: (i, j)),
        out_shape=jax.ShapeDtypeStruct((n, m), F32),
        compiler_params=_params(("parallel", "parallel", "arbitrary")),
        name="matmul_residual",
    )(a, w, res)


def _softplus(z):
    return jnp.maximum(z, 0.0) + jnp.log(1.0 + jnp.exp(-jnp.abs(z)))


def _attn_kernel(q_ref, k_ref, v_ref, o_ref, *, tq, scale):
    seq = q_ref.shape[1]
    nq = seq // tq
    row = lax.broadcasted_iota(jnp.int32, (tq, tq), 0)
    col = lax.broadcasted_iota(jnp.int32, (tq, tq), 1)
    strict = col < row
    tri = (row >= col).astype(BF16)

    def suffix_sum(sp):
        hi = sp.astype(BF16)
        lo = (sp - hi.astype(F32)).astype(BF16)
        return (jnp.dot(hi, tri, preferred_element_type=F32)
                + jnp.dot(lo, tri, preferred_element_type=F32))

    def scores(q, kj):
        k = k_ref[0, pl.ds(pl.multiple_of(kj * tq, tq), tq), :]
        z = lax.dot_general(q, k, (((1,), (1,)), ((), ())), preferred_element_type=F32)
        return z * scale

    def q_block(qi, _):
        q0 = pl.multiple_of(qi * tq, tq)
        q = q_ref[0, pl.ds(q0, tq), :]

        z = scores(q, qi)
        sp = jnp.where(strict, _softplus(z), 0.0)
        cs = suffix_sum(sp)
        a = jnp.where(strict, jnp.exp(z - cs), 0.0)
        v = v_ref[0, pl.ds(q0, tq), :]
        acc = jnp.dot(a.astype(BF16), v, preferred_element_type=F32)
        carry = jnp.broadcast_to(cs[:, 0:1], (tq, HEAD_DIM))

        def k_block(step, state):
            acc, carry = state
            kj = qi - 1 - step
            z = scores(q, kj)
            cs = suffix_sum(_softplus(z))
            a = jnp.exp(z - cs - jnp.tile(carry, (1, tq // HEAD_DIM)))
            v = v_ref[0, pl.ds(pl.multiple_of(kj * tq, tq), tq), :]
            acc = acc + jnp.dot(a.astype(BF16), v, preferred_element_type=F32)
            carry = carry + jnp.broadcast_to(cs[:, 0:1], (tq, HEAD_DIM))
            return acc, carry

        acc, _ = lax.fori_loop(0, qi, k_block, (acc, carry))
        o_ref[0, pl.ds(q0, tq), :] = acc.astype(o_ref.dtype)
        return 0

    lax.fori_loop(0, nq, q_block, 0)


def stick_breaking_attention(qkv, n_heads, head_dim, tq=256):
    b, s, _ = qkv.shape
    blk = lambda off: pl.BlockSpec((1, s, head_dim), lambda bi, hi: (bi, 0, off + hi))
    return pl.pallas_call(
        functools.partial(_attn_kernel, tq=tq, scale=head_dim ** -0.5),
        grid=(b, n_heads),
        in_specs=[blk(0), blk(n_heads), blk(2 * n_heads)],
        out_specs=blk(0),
        out_shape=jax.ShapeDtypeStruct((b, s, n_heads * head_dim), BF16),
        compiler_params=_params(("parallel", "parallel")),
        name="stick_breaking_attention",
    )(qkv, qkv, qkv)


HALO_ROWS = 16


def _conv_kernel(gb_ref, gc_ref, h_ref, gcp_ref, hp_ref, w_ref, o_ref):
    ts = gc_ref.shape[1]
    u = gc_ref[0].astype(F32) * h_ref[0].astype(F32)
    first = pl.program_id(1) == 0
    u_prev = jnp.where(first, 0.0, gcp_ref[0].astype(F32) * hp_ref[0].astype(F32))
    ext = jnp.concatenate([u_prev, u], axis=0)
    w = w_ref[...]
    y = u * w[CONV_WIDTH - 1:CONV_WIDTH, :]
    for lag in range(1, CONV_WIDTH):
        shifted = pltpu.roll(ext, lag, axis=0)[HALO_ROWS:HALO_ROWS + ts]
        y = y + shifted * w[CONV_WIDTH - 1 - lag:CONV_WIDTH - lag, :]
    o_ref[0] = (gb_ref[0].astype(F32) * y).astype(o_ref.dtype)


def short_gated_conv_core(proj, conv_w, ts=1024, td=512):
    b, s, d3 = proj.shape
    d = d3 // 3
    nd = d // td
    blk = lambda off: pl.BlockSpec((1, ts, td), lambda bi, si, j: (bi, si, off + j))
    halo = lambda off: pl.BlockSpec(
        (1, HALO_ROWS, td), lambda bi, si, j: (bi, jnp.maximum(si * (ts // HALO_ROWS) - 1, 0), off + j))
    return pl.pallas_call(
        _conv_kernel,
        grid=(b, s // ts, nd),
        in_specs=[blk(0), blk(nd), blk(2 * nd), halo(nd), halo(2 * nd),
                  pl.BlockSpec((CONV_WIDTH, td), lambda bi, si, j: (0, j))],
        out_specs=blk(0),
        out_shape=jax.ShapeDtypeStruct((b, s, d), BF16),
        compiler_params=_params(("parallel", "parallel", "parallel")),
        name="short_gated_conv",
    )(proj, proj, proj, proj, proj, conv_w)


def _ffn(x, g, wi, wo):
    d, f2 = wi.shape
    f = f2 // 2
    fp = -(-f // 1024) * 1024
    wg = jnp.pad(wi[:, :f].astype(BF16), ((0, 0), (0, fp - f)))
    wu = jnp.pad(wi[:, f:].astype(BF16), ((0, 0), (0, fp - f)))
    wo_p = jnp.pad(wo.astype(BF16), ((0, fp - f), (0, 0)))
    h = rmsnorm(x, g, BF16)
    a = matmul_swiglu(h, wg, wu)
    return matmul_residual(a, wo_p, x, FFN_HALF)


def kernel(x, ffn_norm, ffn_wi, ffn_wo, mix_norm, attn_wqkv, attn_wo, conv_win, conv_w, conv_wout, final_norm):
    b, s, d = x.shape
    depth = ffn_norm.shape[0]
    xf = x.reshape(b * s, d)
    for i in range(depth):
        xf = _ffn(xf, ffn_norm[i, 0], ffn_wi[i, 0], ffn_wo[i, 0])
        h = rmsnorm(xf, mix_norm[i], BF16)
        j = i // 2
        if i % 2 == 0:
            qkv = matmul(h, attn_wqkv[j].astype(BF16), BF16)
            o = stick_breaking_attention(qkv.reshape(b, s, 3 * d), N_HEADS, HEAD_DIM)
            xf = matmul_residual(o.reshape(b * s, d), attn_wo[j].astype(BF16), xf, 1.0)
        else:
            proj = matmul(h, conv_win[j].astype(BF16), BF16)
            y = short_gated_conv_core(proj.reshape(b, s, 3 * d), conv_w[j])
            xf = matmul_residual(y.reshape(b * s, d), conv_wout[j].astype(BF16), xf, 1.0)
        xf = _ffn(xf, ffn_norm[i, 1], ffn_wi[i, 1], ffn_wo[i, 1])
    return rmsnorm(xf, final_norm, F32).reshape(b, s, d)
```

```python
import functools
import math

import jax
import jax.numpy as jnp
from jax import lax
from jax.experimental import pallas as pl
from jax.experimental.pallas import tpu as pltpu

N_HEADS = 32
HEAD_DIM = 128
CONV_WIDTH = 3
FFN_HALF = 0.5
NORM_EPS = 1e-6

V7X_VMEM_BYTES = 64 * 1024 * 1024
VMEM_LIMIT_BYTES = 56 * 1024 * 1024

F32 = jnp.float32
BF16 = jnp.bfloat16


def _params(semantics):
    return pltpu.CompilerParams(dimension_semantics=semantics, vmem_limit_bytes=VMEM_LIMIT_BYTES)


def _rmsnorm_kernel(x_ref, g_ref, o_ref):
    x = x_ref[...]
    ms = jnp.mean(x * x, axis=-1, keepdims=True)
    o_ref[...] = (x * lax.rsqrt(ms + NORM_EPS) * g_ref[...]).astype(o_ref.dtype)


def rmsnorm(x, g, out_dtype, tm=512):
    n, d = x.shape
    return pl.pallas_call(
        _rmsnorm_kernel,
        grid=(n // tm,),
        in_specs=[pl.BlockSpec((tm, d), lambda i: (i, 0)), pl.BlockSpec((1, d), lambda i: (0, 0))],
        out_specs=pl.BlockSpec((tm, d), lambda i: (i, 0)),
        out_shape=jax.ShapeDtypeStruct((n, d), out_dtype),
        compiler_params=_params(("parallel",)),
        name="rmsnorm",
    )(x, g.reshape(1, d))


def _mm_kernel(a_ref, w_ref, o_ref, *, scaled_blocks, col_scale):
    acc = jnp.dot(a_ref[...], w_ref[...], preferred_element_type=F32)
    if scaled_blocks:
        acc = acc * jnp.where(pl.program_id(1) < scaled_blocks, col_scale, 1.0)
    o_ref[...] = acc.astype(o_ref.dtype)


def matmul(a, w, out_dtype, tm=1024, tn=1024, scaled_cols=0, col_scale=1.0):
    n, k = a.shape
    m = w.shape[1]
    assert scaled_cols % tn == 0
    return pl.pallas_call(
        functools.partial(_mm_kernel, scaled_blocks=scaled_cols // tn, col_scale=col_scale),
        grid=(n // tm, m // tn),
        in_specs=[pl.BlockSpec((tm, k), lambda i, j: (i, 0)), pl.BlockSpec((k, tn), lambda i, j: (0, j))],
        out_specs=pl.BlockSpec((tm, tn), lambda i, j: (i, j)),
        out_shape=jax.ShapeDtypeStruct((n, m), out_dtype),
        compiler_params=_params(("parallel", "arbitrary")),
        name="matmul",
    )(a, w)


def _swiglu_kernel(h_ref, wg_ref, wu_ref, o_ref):
    h = h_ref[...]
    gate = jnp.dot(h, wg_ref[...], preferred_element_type=F32)
    up = jnp.dot(h, wu_ref[...], preferred_element_type=F32)
    o_ref[...] = (gate * jax.nn.sigmoid(gate) * up).astype(o_ref.dtype)


def matmul_swiglu(h, wi, tm=1024, tf=256):
    n, k = h.shape
    f = wi.shape[1] // 2
    nf = f // tf
    return pl.pallas_call(
        _swiglu_kernel,
        grid=(n // tm, nf),
        in_specs=[
            pl.BlockSpec((tm, k), lambda i, j: (i, 0)),
            pl.BlockSpec((k, tf), lambda i, j: (0, j)),
            pl.BlockSpec((k, tf), lambda i, j: (0, nf + j)),
        ],
        out_specs=pl.BlockSpec((tm, tf), lambda i, j: (i, j)),
        out_shape=jax.ShapeDtypeStruct((n, f), BF16),
        compiler_params=_params(("parallel", "arbitrary")),
        name="matmul_swiglu",
    )(h, wi, wi)


def _mm_residual_kernel(a_ref, w_ref, r_ref, o_ref, *, scale, nk):
    part = scale * jnp.dot(a_ref[...], w_ref[...], preferred_element_type=F32)
    if nk == 1:
        o_ref[...] = r_ref[...] + part
        return
    kk = pl.program_id(2)

    @pl.when(kk == 0)
    def _():
        o_ref[...] = r_ref[...] + part

    @pl.when(kk != 0)
    def _():
        o_ref[...] += part


def matmul_residual(a, w, res, scale, tm, tn, tk):
    n, k = a.shape
    m = w.shape[1]
    nk = k // tk
    return pl.pallas_call(
        functools.partial(_mm_residual_kernel, scale=scale, nk=nk),
        grid=(n // tm, m // tn, nk),
        in_specs=[
            pl.BlockSpec((tm, tk), lambda i, j, kk: (i, kk)),
            pl.BlockSpec((tk, tn), lambda i, j, kk: (kk, j)),
            pl.BlockSpec((tm, tn), lambda i, j, kk: (i, j)),
        ],
        out_specs=pl.BlockSpec((tm, tn), lambda i, j, kk: (i, j)),
        out_shape=jax.ShapeDtypeStruct((n, m), F32),
        compiler_params=_params(("parallel", "parallel", "arbitrary")),
        name="matmul_residual",
    )(a, w, res)


NEG = -1e30
BIG = 1e30
LOG2E = math.log2(math.e)
ATTN_STAGES = 3
ATTN_UNROLL = 8


def _attn_kernel(q_ref, k_ref, v_ref, o_ref, y_scr, hi_scr, a_scr, acc_scr, carry_scr, cap_scr, *, tq, tk):
    seq, hd = q_ref.shape[1], q_ref.shape[2]
    nq, r = seq // tq, tq // tk
    n_tiles = r * nq * (nq + 1) // 2
    n_steps = -(-(n_tiles + ATTN_STAGES - 1) // ATTN_UNROLL) * ATTN_UNROLL

    row = lax.broadcasted_iota(jnp.int32, (tk, tk), 0)
    col = lax.broadcasted_iota(jnp.int32, (tk, tk), 1)
    tri = (row >= col).astype(BF16)
    col_minus_row = (lax.broadcasted_iota(jnp.int32, (tq, tk), 1)
                     - lax.broadcasted_iota(jnp.int32, (tq, tk), 0))
    for d in range(r):
        cap_scr[d] = jnp.where(col_minus_row < -d * tk, BIG, NEG)
    cap_scr[r] = jnp.full((tq, tk), BIG, F32)
    cap_scr[r + 1] = jnp.full((tq, tk), NEG, F32)

    acc_scr[...] = jnp.zeros_like(acc_scr)
    carry_scr[...] = jnp.zeros_like(carry_scr)
    y_scr[1] = jnp.full((tq, tk), NEG, F32)
    hi_scr[1] = jnp.zeros((tq, tk), BF16)
    a_scr[1] = jnp.zeros((tq, tk), BF16)

    def q_rows(qi):
        return pl.ds(pl.multiple_of(qi * tq, tq), tq)

    def k_rows(kb):
        return pl.ds(pl.multiple_of(kb * tk, tk), tk)

    def step(cur, t, tiles):
        prev = 1 - cur
        (qi1, kb1), (qi2, _), (qi3, kb3) = tiles

        acc_scr[q_rows(qi3), :] += jnp.dot(a_scr[prev], v_ref[0, k_rows(kb3), :], preferred_element_type=F32)

        c = carry_scr[q_rows(qi2), :]
        cs = jnp.dot(hi_scr[prev], tri, preferred_element_type=F32)
        a_scr[cur] = jnp.exp2(y_scr[prev] - jnp.tile(c, (1, tk // hd)) - cs).astype(BF16)
        carry_scr[q_rows(qi2), :] = c + jnp.broadcast_to(cs[:, 0:1], (tq, hd))

        y = lax.dot_general(q_ref[0, q_rows(qi1), :], k_ref[0, k_rows(kb1), :], (((1,), (1,)), ((), ())),
                            preferred_element_type=F32)
        off = kb1 - r * qi1
        cap = jnp.where(t < n_tiles, jnp.where(off < 0, r, off), r + 1)
        y = jnp.minimum(y, cap_scr[cap])
        sp = jnp.maximum(y, jnp.log2(1.0 + jnp.exp2(jnp.minimum(y, 64.0))))
        hi_scr[cur] = sp.astype(BF16)
        y_scr[cur] = y

        last = kb1 == 0
        qn = jnp.where(last, jnp.minimum(qi1 + 1, nq - 1), qi1)
        kn = jnp.where(last, r * qn + r - 1, kb1 - 1)
        return t + 1, ((qn, kn),) + tiles[:-1]

    def body(_, st):
        t, tiles = st
        for u in range(ATTN_UNROLL):
            t, tiles = step(u % 2, t, tiles)
        return t, tiles

    zero = jnp.int32(0)
    init = (zero, ((zero, jnp.int32(r - 1)),) + ((zero, zero),) * (ATTN_STAGES - 1))
    lax.fori_loop(0, n_steps // ATTN_UNROLL, body, init)
    o_ref[0] = acc_scr[...].astype(o_ref.dtype)


def stick_breaking_attention(qkv, n_heads, head_dim, tq=512, tk=256):
    b, s, _ = qkv.shape
    assert ATTN_UNROLL % 2 == 0 and tq % tk == 0 and tk % head_dim == 0 and s % tq == 0
    blk = lambda off: pl.BlockSpec((1, s, head_dim), lambda bi, hi: (bi, 0, off + hi))
    return pl.pallas_call(
        functools.partial(_attn_kernel, tq=tq, tk=tk),
        grid=(b, n_heads),
        in_specs=[blk(0), blk(n_heads), blk(2 * n_heads)],
        out_specs=blk(0),
        out_shape=jax.ShapeDtypeStruct((b, s, n_heads * head_dim), BF16),
        scratch_shapes=[
            pltpu.VMEM((2, tq, tk), F32),
            pltpu.VMEM((2, tq, tk), BF16),
            pltpu.VMEM((2, tq, tk), BF16),
            pltpu.VMEM((s, head_dim), F32),
            pltpu.VMEM((s, head_dim), F32),
            pltpu.VMEM((tq // tk + 2, tq, tk), F32),
        ],
        compiler_params=_params(("parallel", "parallel")),
        name="stick_breaking_attention",
    )(qkv, qkv, qkv)


HALO_ROWS = 16


def _conv_kernel(gb_ref, gc_ref, h_ref, gcp_ref, hp_ref, w_ref, o_ref):
    ts = gc_ref.shape[1]
    u = gc_ref[0].astype(F32) * h_ref[0].astype(F32)
    first = pl.program_id(1) == 0
    u_prev = jnp.where(first, 0.0, gcp_ref[0].astype(F32) * hp_ref[0].astype(F32))
    ext = jnp.concatenate([u_prev, u], axis=0)
    w = w_ref[...]
    y = u * w[CONV_WIDTH - 1:CONV_WIDTH, :]
    for lag in range(1, CONV_WIDTH):
        shifted = pltpu.roll(ext, lag, axis=0)[HALO_ROWS:HALO_ROWS + ts]
        y = y + shifted * w[CONV_WIDTH - 1 - lag:CONV_WIDTH - lag, :]
    o_ref[0] = (gb_ref[0].astype(F32) * y).astype(o_ref.dtype)


def short_gated_conv_core(proj, conv_w, ts=1024, td=512):
    b, s, d3 = proj.shape
    d = d3 // 3
    nd = d // td
    blk = lambda off: pl.BlockSpec((1, ts, td), lambda bi, si, j: (bi, si, off + j))
    halo = lambda off: pl.BlockSpec(
        (1, HALO_ROWS, td), lambda bi, si, j: (bi, jnp.maximum(si * (ts // HALO_ROWS) - 1, 0), off + j))
    return pl.pallas_call(
        _conv_kernel,
        grid=(b, s // ts, nd),
        in_specs=[blk(0), blk(nd), blk(2 * nd), halo(nd), halo(2 * nd),
                  pl.BlockSpec((CONV_WIDTH, td), lambda bi, si, j: (0, j))],
        out_specs=blk(0),
        out_shape=jax.ShapeDtypeStruct((b, s, d), BF16),
        compiler_params=_params(("parallel", "parallel", "parallel")),
        name="short_gated_conv",
    )(proj, proj, proj, proj, proj, conv_w)


def _ffn(x, g, wi, wo):
    f = wo.shape[0]
    h = rmsnorm(x, g, BF16)
    a = matmul_swiglu(h, wi.astype(BF16))
    return matmul_residual(a, wo.astype(BF16), x, FFN_HALF, tm=512, tn=1024, tk=f // 2)


def kernel(x, ffn_norm, ffn_wi, ffn_wo, mix_norm, attn_wqkv, attn_wo, conv_win, conv_w, conv_wout, final_norm):
    b, s, d = x.shape
    depth = ffn_norm.shape[0]
    xf = x.reshape(b * s, d)
    for i in range(depth):
        xf = _ffn(xf, ffn_norm[i, 0], ffn_wi[i, 0], ffn_wo[i, 0])
        h = rmsnorm(xf, mix_norm[i], BF16)
        j = i // 2
        if i % 2 == 0:
            qkv = matmul(h, attn_wqkv[j].astype(BF16), BF16, scaled_cols=d, col_scale=LOG2E * HEAD_DIM ** -0.5)
            o = stick_breaking_attention(qkv.reshape(b, s, 3 * d), N_HEADS, HEAD_DIM)
            xf = matmul_residual(o.reshape(b * s, d), attn_wo[j].astype(BF16), xf, 1.0, tm=1024, tn=1024, tk=d)
        else:
            proj = matmul(h, conv_win[j].astype(BF16), BF16)
            y = short_gated_conv_core(proj.reshape(b, s, 3 * d), conv_w[j])
            xf = matmul_residual(y.reshape(b * s, d), conv_wout[j].astype(BF16), xf, 1.0, tm=1024, tn=1024, tk=d)
        xf = _ffn(xf, ffn_norm[i, 1], ffn_wi[i, 1], ffn_wo[i, 1])
    return rmsnorm(xf, final_norm, F32).reshape(b, s, d)
```

```python
import functools
import math

import jax
import jax.numpy as jnp
from jax import lax
from jax.experimental import pallas as pl
from jax.experimental.pallas import tpu as pltpu

N_HEADS = 32
HEAD_DIM = 128
CONV_WIDTH = 3
FFN_HALF = 0.5
NORM_EPS = 1e-6

V7X_VMEM_BYTES = 64 * 1024 * 1024
VMEM_LIMIT_BYTES = 56 * 1024 * 1024

F32 = jnp.float32
BF16 = jnp.bfloat16


def _params(semantics):
    return pltpu.CompilerParams(dimension_semantics=semantics, vmem_limit_bytes=VMEM_LIMIT_BYTES)


def _rmsnorm_kernel(x_ref, g_ref, o_ref):
    x = x_ref[...]
    ms = jnp.mean(x * x, axis=-1, keepdims=True)
    o_ref[...] = (x * lax.rsqrt(ms + NORM_EPS) * g_ref[...]).astype(o_ref.dtype)


def rmsnorm(x, g, out_dtype, tm=512):
    n, d = x.shape
    return pl.pallas_call(
        _rmsnorm_kernel,
        grid=(n // tm,),
        in_specs=[pl.BlockSpec((tm, d), lambda i: (i, 0)), pl.BlockSpec((1, d), lambda i: (0, 0))],
        out_specs=pl.BlockSpec((tm, d), lambda i: (i, 0)),
        out_shape=jax.ShapeDtypeStruct((n, d), out_dtype),
        compiler_params=_params(("parallel",)),
        name="rmsnorm",
    )(x, g.reshape(1, d))


def _mm_kernel(a_ref, w_ref, o_ref, *, scaled_blocks, col_scale):
    acc = jnp.dot(a_ref[...], w_ref[...], preferred_element_type=F32)
    if scaled_blocks:
        acc = acc * jnp.where(pl.program_id(1) < scaled_blocks, col_scale, 1.0)
    o_ref[...] = acc.astype(o_ref.dtype)


def matmul(a, w, out_dtype, tm=1024, tn=1024, scaled_cols=0, col_scale=1.0):
    n, k = a.shape
    m = w.shape[1]
    assert scaled_cols % tn == 0
    return pl.pallas_call(
        functools.partial(_mm_kernel, scaled_blocks=scaled_cols // tn, col_scale=col_scale),
        grid=(n // tm, m // tn),
        in_specs=[pl.BlockSpec((tm, k), lambda i, j: (i, 0)), pl.BlockSpec((k, tn), lambda i, j: (0, j))],
        out_specs=pl.BlockSpec((tm, tn), lambda i, j: (i, j)),
        out_shape=jax.ShapeDtypeStruct((n, m), out_dtype),
        compiler_params=_params(("parallel", "arbitrary")),
        name="matmul",
    )(a, w)


def _swiglu_kernel(h_ref, wg_ref, wu_ref, o_ref, wg_bf, wu_bf):
    @pl.when(pl.program_id(1) == 0)
    def _():
        wg_bf[...] = wg_ref[...].astype(BF16)
        wu_bf[...] = wu_ref[...].astype(BF16)

    h = h_ref[...]
    gate = jnp.dot(h, wg_bf[...], preferred_element_type=F32)
    up = jnp.dot(h, wu_bf[...], preferred_element_type=F32)
    o_ref[...] = (gate * jax.nn.sigmoid(gate) * up).astype(o_ref.dtype)


def matmul_swiglu(h, wi, layer, slot, tm=1024, tf=256):
    n, k = h.shape
    f = wi.shape[-1] // 2
    nf = f // tf
    return pl.pallas_call(
        _swiglu_kernel,
        grid=(nf, n // tm),
        in_specs=[
            pl.BlockSpec((tm, k), lambda j, i: (i, 0)),
            pl.BlockSpec((None, None, k, tf), lambda j, i: (layer, slot, 0, j)),
            pl.BlockSpec((None, None, k, tf), lambda j, i: (layer, slot, 0, nf + j)),
        ],
        out_specs=pl.BlockSpec((tm, tf), lambda j, i: (i, j)),
        out_shape=jax.ShapeDtypeStruct((n, f), BF16),
        scratch_shapes=[pltpu.VMEM((k, tf), BF16), pltpu.VMEM((k, tf), BF16)],
        compiler_params=_params(("arbitrary", "arbitrary")),
        name="matmul_swiglu",
    )(h, wi, wi)


def _mm_residual_kernel(a_ref, w_ref, r_ref, o_ref, *, scale, nk):
    part = scale * jnp.dot(a_ref[...], w_ref[...], preferred_element_type=F32)
    if nk == 1:
        o_ref[...] = r_ref[...] + part
        return
    kk = pl.program_id(2)

    @pl.when(kk == 0)
    def _():
        o_ref[...] = r_ref[...] + part

    @pl.when(kk != 0)
    def _():
        o_ref[...] += part


def matmul_residual(a, w, res, scale, tm, tn, tk):
    n, k = a.shape
    m = w.shape[1]
    nk = k // tk
    return pl.pallas_call(
        functools.partial(_mm_residual_kernel, scale=scale, nk=nk),
        grid=(n // tm, m // tn, nk),
        in_specs=[
            pl.BlockSpec((tm, tk), lambda i, j, kk: (i, kk)),
            pl.BlockSpec((tk, tn), lambda i, j, kk: (kk, j)),
            pl.BlockSpec((tm, tn), lambda i, j, kk: (i, j)),
        ],
        out_specs=pl.BlockSpec((tm, tn), lambda i, j, kk: (i, j)),
        out_shape=jax.ShapeDtypeStruct((n, m), F32),
        compiler_params=_params(("parallel", "parallel", "arbitrary")),
        name="matmul_residual",
    )(a, w, res)


NEG = -1e30
YMAX = 64.0
LOG2E = math.log2(math.e)
ATTN_STAGES = 3
ATTN_UNROLL = 8


def _attn_kernel(q_ref, k_ref, v_ref, o_ref, y_scr, hi_scr, a_scr, acc_scr, carry_scr, cap_scr, *, tq, tk):
    seq, hd = q_ref.shape[1], q_ref.shape[2]
    nq, r = seq // tq, tq // tk
    n_tiles = r * nq * (nq + 1) // 2
    n_steps = -(-(n_tiles + ATTN_STAGES - 1) // ATTN_UNROLL) * ATTN_UNROLL

    row = lax.broadcasted_iota(jnp.int32, (tk, tk), 0)
    col = lax.broadcasted_iota(jnp.int32, (tk, tk), 1)
    tri = (row >= col).astype(BF16)
    col_minus_row = (lax.broadcasted_iota(jnp.int32, (tq, tk), 1)
                     - lax.broadcasted_iota(jnp.int32, (tq, tk), 0))
    for d in range(r):
        cap_scr[d] = jnp.where(col_minus_row < -d * tk, YMAX, NEG)
    cap_scr[r] = jnp.full((tq, tk), YMAX, F32)
    cap_scr[r + 1] = jnp.full((tq, tk), NEG, F32)

    acc_scr[...] = jnp.zeros_like(acc_scr)
    carry_scr[...] = jnp.zeros_like(carry_scr)
    y_scr[1] = jnp.full((tq, tk), NEG, F32)
    hi_scr[1] = jnp.zeros((tq, tk), BF16)
    a_scr[1] = jnp.zeros((tq, tk), BF16)

    def q_rows(qi):
        return pl.ds(pl.multiple_of(qi * tq, tq), tq)

    def k_rows(kb):
        return pl.ds(pl.multiple_of(kb * tk, tk), tk)

    def step(cur, t, tiles):
        prev = 1 - cur
        (qi1, kb1), (qi2, _), (qi3, kb3) = tiles

        acc_scr[q_rows(qi3), :] += jnp.dot(a_scr[prev], v_ref[0, k_rows(kb3), :], preferred_element_type=F32)

        c = carry_scr[q_rows(qi2), :]
        cs = jnp.dot(hi_scr[prev], tri, preferred_element_type=F32)
        a_scr[cur] = jnp.exp2(y_scr[prev] - jnp.tile(c, (1, tk // hd)) - cs).astype(BF16)
        carry_scr[q_rows(qi2), :] = c + jnp.broadcast_to(cs[:, 0:1], (tq, hd))

        y = lax.dot_general(q_ref[0, q_rows(qi1), :], k_ref[0, k_rows(kb1), :], (((1,), (1,)), ((), ())),
                            preferred_element_type=F32)
        off = kb1 - r * qi1
        cap = jnp.where(t < n_tiles, jnp.where(off < 0, r, off), r + 1)
        y = jnp.minimum(y, cap_scr[cap])
        sp = jnp.log2(1.0 + jnp.exp2(y))
        hi_scr[cur] = sp.astype(BF16)
        y_scr[cur] = y

        last = kb1 == 0
        qn = jnp.where(last, jnp.minimum(qi1 + 1, nq - 1), qi1)
        kn = jnp.where(last, r * qn + r - 1, kb1 - 1)
        return t + 1, ((qn, kn),) + tiles[:-1]

    def body(_, st):
        t, tiles = st
        for u in range(ATTN_UNROLL):
            t, tiles = step(u % 2, t, tiles)
        return t, tiles

    zero = jnp.int32(0)
    init = (zero, ((zero, jnp.int32(r - 1)),) + ((zero, zero),) * (ATTN_STAGES - 1))
    lax.fori_loop(0, n_steps // ATTN_UNROLL, body, init)
    o_ref[0] = acc_scr[...].astype(o_ref.dtype)


def stick_breaking_attention(qkv, n_heads, head_dim, tq=512, tk=256):
    b, s, _ = qkv.shape
    assert ATTN_UNROLL % 2 == 0 and tq % tk == 0 and tk % head_dim == 0 and s % tq == 0
    blk = lambda off: pl.BlockSpec((1, s, head_dim), lambda bi, hi: (bi, 0, off + hi))
    return pl.pallas_call(
        functools.partial(_attn_kernel, tq=tq, tk=tk),
        grid=(b, n_heads),
        in_specs=[blk(0), blk(n_heads), blk(2 * n_heads)],
        out_specs=blk(0),
        out_shape=jax.ShapeDtypeStruct((b, s, n_heads * head_dim), BF16),
        scratch_shapes=[
            pltpu.VMEM((2, tq, tk), F32),
            pltpu.VMEM((2, tq, tk), BF16),
            pltpu.VMEM((2, tq, tk), BF16),
            pltpu.VMEM((s, head_dim), F32),
            pltpu.VMEM((s, head_dim), F32),
            pltpu.VMEM((tq // tk + 2, tq, tk), F32),
        ],
        compiler_params=_params(("parallel", "parallel")),
        name="stick_breaking_attention",
    )(qkv, qkv, qkv)


HALO_ROWS = 16


def _conv_kernel(gb_ref, gc_ref, h_ref, gcp_ref, hp_ref, w_ref, o_ref):
    ts = gc_ref.shape[1]
    u = gc_ref[0].astype(F32) * h_ref[0].astype(F32)
    first = pl.program_id(1) == 0
    u_prev = jnp.where(first, 0.0, gcp_ref[0].astype(F32) * hp_ref[0].astype(F32))
    ext = jnp.concatenate([u_prev, u], axis=0)
    w = w_ref[...]
    y = u * w[CONV_WIDTH - 1:CONV_WIDTH, :]
    for lag in range(1, CONV_WIDTH):
        shifted = pltpu.roll(ext, lag, axis=0)[HALO_ROWS:HALO_ROWS + ts]
        y = y + shifted * w[CONV_WIDTH - 1 - lag:CONV_WIDTH - lag, :]
    o_ref[0] = (gb_ref[0].astype(F32) * y).astype(o_ref.dtype)


def short_gated_conv_core(proj, conv_w, ts=1024, td=512):
    b, s, d3 = proj.shape
    d = d3 // 3
    nd = d // td
    blk = lambda off: pl.BlockSpec((1, ts, td), lambda bi, si, j: (bi, si, off + j))
    halo = lambda off: pl.BlockSpec(
        (1, HALO_ROWS, td), lambda bi, si, j: (bi, jnp.maximum(si * (ts // HALO_ROWS) - 1, 0), off + j))
    return pl.pallas_call(
        _conv_kernel,
        grid=(b, s // ts, nd),
        in_specs=[blk(0), blk(nd), blk(2 * nd), halo(nd), halo(2 * nd),
                  pl.BlockSpec((CONV_WIDTH, td), lambda bi, si, j: (0, j))],
        out_specs=blk(0),
        out_shape=jax.ShapeDtypeStruct((b, s, d), BF16),
        compiler_params=_params(("parallel", "parallel", "parallel")),
        name="short_gated_conv",
    )(proj, proj, proj, proj, proj, conv_w)


def _ffn(x, g, wi, wo, layer, slot):
    f = wo.shape[-2]
    h = rmsnorm(x, g[layer, slot], BF16)
    a = matmul_swiglu(h, wi, layer, slot)
    return matmul_residual(a, wo[layer, slot].astype(BF16), x, FFN_HALF, tm=512, tn=1024, tk=f // 2)


def kernel(x, ffn_norm, ffn_wi, ffn_wo, mix_norm, attn_wqkv, attn_wo, conv_win, conv_w, conv_wout, final_norm):
    b, s, d = x.shape
    depth = ffn_norm.shape[0]
    xf = x.reshape(b * s, d)
    for i in range(depth):
        xf = _ffn(xf, ffn_norm, ffn_wi, ffn_wo, i, 0)
        h = rmsnorm(xf, mix_norm[i], BF16)
        j = i // 2
        if i % 2 == 0:
            qkv = matmul(h, attn_wqkv[j].astype(BF16), BF16, scaled_cols=d, col_scale=LOG2E * HEAD_DIM ** -0.5)
            o = stick_breaking_attention(qkv.reshape(b, s, 3 * d), N_HEADS, HEAD_DIM)
            xf = matmul_residual(o.reshape(b * s, d), attn_wo[j].astype(BF16), xf, 1.0, tm=1024, tn=1024, tk=d)
        else:
            proj = matmul(h, conv_win[j].astype(BF16), BF16)
            y = short_gated_conv_core(proj.reshape(b, s, 3 * d), conv_w[j])
            xf = matmul_residual(y.reshape(b * s, d), conv_wout[j].astype(BF16), xf, 1.0, tm=1024, tn=1024, tk=d)
        xf = _ffn(xf, ffn_norm, ffn_wi, ffn_wo, i, 1)
    return rmsnorm(xf, final_norm, F32).reshape(b, s, d)
```

```python
import functools
import math

import jax
import jax.numpy as jnp
from jax import lax
from jax.experimental import pallas as pl
from jax.experimental.pallas import tpu as pltpu

N_HEADS = 32
HEAD_DIM = 128
CONV_WIDTH = 3
FFN_HALF = 0.5
NORM_EPS = 1e-6

V7X_VMEM_BYTES = 64 * 1024 * 1024
VMEM_LIMIT_BYTES = 56 * 1024 * 1024

F32 = jnp.float32
BF16 = jnp.bfloat16


def _params(semantics):
    return pltpu.CompilerParams(dimension_semantics=semantics, vmem_limit_bytes=VMEM_LIMIT_BYTES)


def _rmsnorm_kernel(x_ref, g_ref, o_ref):
    x = x_ref[...]
    ms = jnp.mean(x * x, axis=-1, keepdims=True)
    o_ref[...] = (x * lax.rsqrt(ms + NORM_EPS) * g_ref[...]).astype(o_ref.dtype)


def rmsnorm(x, g, out_dtype, tm=512):
    n, d = x.shape
    return pl.pallas_call(
        _rmsnorm_kernel,
        grid=(n // tm,),
        in_specs=[pl.BlockSpec((tm, d), lambda i: (i, 0)), pl.BlockSpec((1, d), lambda i: (0, 0))],
        out_specs=pl.BlockSpec((tm, d), lambda i: (i, 0)),
        out_shape=jax.ShapeDtypeStruct((n, d), out_dtype),
        compiler_params=_params(("parallel",)),
        name="rmsnorm",
    )(x, g.reshape(1, d))


def _mm_kernel(a_ref, w_ref, o_ref, *, scaled_blocks, col_scale):
    acc = jnp.dot(a_ref[...], w_ref[...], preferred_element_type=F32)
    if scaled_blocks:
        acc = acc * jnp.where(pl.program_id(1) < scaled_blocks, col_scale, 1.0)
    o_ref[...] = acc.astype(o_ref.dtype)


def matmul(a, w, out_dtype, tm=1024, tn=1024, scaled_cols=0, col_scale=1.0):
    n, k = a.shape
    m = w.shape[1]
    assert scaled_cols % tn == 0
    return pl.pallas_call(
        functools.partial(_mm_kernel, scaled_blocks=scaled_cols // tn, col_scale=col_scale),
        grid=(n // tm, m // tn),
        in_specs=[pl.BlockSpec((tm, k), lambda i, j: (i, 0)), pl.BlockSpec((k, tn), lambda i, j: (0, j))],
        out_specs=pl.BlockSpec((tm, tn), lambda i, j: (i, j)),
        out_shape=jax.ShapeDtypeStruct((n, m), out_dtype),
        compiler_params=_params(("parallel", "arbitrary")),
        name="matmul",
    )(a, w)


def _swiglu_kernel(h_ref, wg_ref, wu_ref, o_ref, wg_bf, wu_bf):
    @pl.when(pl.program_id(1) == 0)
    def _():
        wg_bf[...] = wg_ref[...].astype(BF16)
        wu_bf[...] = wu_ref[...].astype(BF16)

    h = h_ref[...]
    gate = jnp.dot(h, wg_bf[...], preferred_element_type=F32)
    up = jnp.dot(h, wu_bf[...], preferred_element_type=F32)
    o_ref[...] = (gate * jax.nn.sigmoid(gate) * up).astype(o_ref.dtype)


def matmul_swiglu(h, wi, layer, slot, tm=1024, tf=256):
    n, k = h.shape
    f = wi.shape[-1] // 2
    nf = f // tf
    return pl.pallas_call(
        _swiglu_kernel,
        grid=(nf, n // tm),
        in_specs=[
            pl.BlockSpec((tm, k), lambda j, i: (i, 0)),
            pl.BlockSpec((None, None, k, tf), lambda j, i: (layer, slot, 0, j)),
            pl.BlockSpec((None, None, k, tf), lambda j, i: (layer, slot, 0, nf + j)),
        ],
        out_specs=pl.BlockSpec((tm, tf), lambda j, i: (i, j)),
        out_shape=jax.ShapeDtypeStruct((n, f), BF16),
        scratch_shapes=[pltpu.VMEM((k, tf), BF16), pltpu.VMEM((k, tf), BF16)],
        compiler_params=_params(("arbitrary", "arbitrary")),
        name="matmul_swiglu",
    )(h, wi, wi)


def _mm_residual_kernel(a_ref, w_ref, r_ref, o_ref, *, scale, nk):
    part = scale * jnp.dot(a_ref[...], w_ref[...], preferred_element_type=F32)
    if nk == 1:
        o_ref[...] = r_ref[...] + part
        return
    kk = pl.program_id(2)

    @pl.when(kk == 0)
    def _():
        o_ref[...] = r_ref[...] + part

    @pl.when(kk != 0)
    def _():
        o_ref[...] += part


def matmul_residual(a, w, res, scale, tm, tn, tk, w_index=()):
    n, k = a.shape
    m = w.shape[-1]
    nk = k // tk
    assert w.ndim == 2 + len(w_index)
    return pl.pallas_call(
        functools.partial(_mm_residual_kernel, scale=scale, nk=nk),
        grid=(n // tm, m // tn, nk),
        in_specs=[
            pl.BlockSpec((tm, tk), lambda i, j, kk: (i, kk)),
            pl.BlockSpec((None,) * len(w_index) + (tk, tn), lambda i, j, kk: tuple(w_index) + (kk, j)),
            pl.BlockSpec((tm, tn), lambda i, j, kk: (i, j)),
        ],
        out_specs=pl.BlockSpec((tm, tn), lambda i, j, kk: (i, j)),
        out_shape=jax.ShapeDtypeStruct((n, m), F32),
        compiler_params=_params(("parallel", "parallel", "arbitrary")),
        name="matmul_residual",
    )(a, w, res)


NEG = -1e30
YMAX = 64.0
LOG2E = math.log2(math.e)
ATTN_STAGES = 3
ATTN_UNROLL = 8


def _attn_kernel(q_ref, k_ref, v_ref, o_ref, y_scr, hi_scr, a_scr, acc_scr, carry_scr, cap_scr, *, tq, tk):
    seq, hd = q_ref.shape[1], q_ref.shape[2]
    nq, r = seq // tq, tq // tk
    n_tiles = r * nq * (nq + 1) // 2
    n_steps = -(-(n_tiles + ATTN_STAGES - 1) // ATTN_UNROLL) * ATTN_UNROLL

    row = lax.broadcasted_iota(jnp.int32, (tk, tk), 0)
    col = lax.broadcasted_iota(jnp.int32, (tk, tk), 1)
    tri = (row >= col).astype(BF16)
    col_minus_row = (lax.broadcasted_iota(jnp.int32, (tq, tk), 1)
                     - lax.broadcasted_iota(jnp.int32, (tq, tk), 0))
    for d in range(r):
        cap_scr[d] = jnp.where(col_minus_row < -d * tk, YMAX, NEG)
    cap_scr[r] = jnp.full((tq, tk), YMAX, F32)
    cap_scr[r + 1] = jnp.full((tq, tk), NEG, F32)

    acc_scr[...] = jnp.zeros_like(acc_scr)
    carry_scr[...] = jnp.zeros_like(carry_scr)
    y_scr[1] = jnp.full((tq, tk), NEG, F32)
    hi_scr[1] = jnp.zeros((tq, tk), BF16)
    a_scr[1] = jnp.zeros((tq, tk), BF16)

    def q_rows(qi):
        return pl.ds(pl.multiple_of(qi * tq, tq), tq)

    def k_rows(kb):
        return pl.ds(pl.multiple_of(kb * tk, tk), tk)

    def step(cur, t, tiles):
        prev = 1 - cur
        (qi1, kb1), (qi2, _), (qi3, kb3) = tiles

        acc_scr[q_rows(qi3), :] += jnp.dot(a_scr[prev], v_ref[0, k_rows(kb3), :], preferred_element_type=F32)

        c = carry_scr[q_rows(qi2), :]
        cs = jnp.dot(hi_scr[prev], tri, preferred_element_type=F32)
        a_scr[cur] = jnp.exp2(y_scr[prev] - jnp.tile(c, (1, tk // hd)) - cs).astype(BF16)
        carry_scr[q_rows(qi2), :] = c + jnp.broadcast_to(cs[:, 0:1], (tq, hd))

        y = lax.dot_general(q_ref[0, q_rows(qi1), :], k_ref[0, k_rows(kb1), :], (((1,), (1,)), ((), ())),
                            preferred_element_type=F32)
        off = kb1 - r * qi1
        cap = jnp.where(t < n_tiles, jnp.where(off < 0, r, off), r + 1)
        y = jnp.minimum(y, cap_scr[cap])
        sp = jnp.log2(1.0 + jnp.exp2(y))
        hi_scr[cur] = sp.astype(BF16)
        y_scr[cur] = y

        last = kb1 == 0
        qn = jnp.where(last, jnp.minimum(qi1 + 1, nq - 1), qi1)
        kn = jnp.where(last, r * qn + r - 1, kb1 - 1)
        return t + 1, ((qn, kn),) + tiles[:-1]

    def body(_, st):
        t, tiles = st
        for u in range(ATTN_UNROLL):
            t, tiles = step(u % 2, t, tiles)
        return t, tiles

    zero = jnp.int32(0)
    init = (zero, ((zero, jnp.int32(r - 1)),) + ((zero, zero),) * (ATTN_STAGES - 1))
    lax.fori_loop(0, n_steps // ATTN_UNROLL, body, init)
    o_ref[0] = acc_scr[...].astype(o_ref.dtype)


def stick_breaking_attention(qkv, n_heads, head_dim, tq=512, tk=256):
    b, s, _ = qkv.shape
    assert ATTN_UNROLL % 2 == 0 and tq % tk == 0 and tk % head_dim == 0 and s % tq == 0
    blk = lambda off: pl.BlockSpec((1, s, head_dim), lambda bi, hi: (bi, 0, off + hi))
    return pl.pallas_call(
        functools.partial(_attn_kernel, tq=tq, tk=tk),
        grid=(b, n_heads),
        in_specs=[blk(0), blk(n_heads), blk(2 * n_heads)],
        out_specs=blk(0),
        out_shape=jax.ShapeDtypeStruct((b, s, n_heads * head_dim), BF16),
        scratch_shapes=[
            pltpu.VMEM((2, tq, tk), F32),
            pltpu.VMEM((2, tq, tk), BF16),
            pltpu.VMEM((2, tq, tk), BF16),
            pltpu.VMEM((s, head_dim), F32),
            pltpu.VMEM((s, head_dim), F32),
            pltpu.VMEM((tq // tk + 2, tq, tk), F32),
        ],
        compiler_params=_params(("parallel", "parallel")),
        name="stick_breaking_attention",
    )(qkv, qkv, qkv)


HALO_ROWS = 16


def _conv_kernel(gb_ref, gc_ref, h_ref, gcp_ref, hp_ref, w_ref, o_ref):
    ts = gc_ref.shape[1]
    u = gc_ref[0].astype(F32) * h_ref[0].astype(F32)
    first = pl.program_id(1) == 0
    u_prev = jnp.where(first, 0.0, gcp_ref[0].astype(F32) * hp_ref[0].astype(F32))
    ext = jnp.concatenate([u_prev, u], axis=0)
    w = w_ref[...]
    y = u * w[CONV_WIDTH - 1:CONV_WIDTH, :]
    for lag in range(1, CONV_WIDTH):
        shifted = pltpu.roll(ext, lag, axis=0)[HALO_ROWS:HALO_ROWS + ts]
        y = y + shifted * w[CONV_WIDTH - 1 - lag:CONV_WIDTH - lag, :]
    o_ref[0] = (gb_ref[0].astype(F32) * y).astype(o_ref.dtype)


def short_gated_conv_core(proj, conv_w, ts=1024, td=512):
    b, s, d3 = proj.shape
    d = d3 // 3
    nd = d // td
    blk = lambda off: pl.BlockSpec((1, ts, td), lambda bi, si, j: (bi, si, off + j))
    halo = lambda off: pl.BlockSpec(
        (1, HALO_ROWS, td), lambda bi, si, j: (bi, jnp.maximum(si * (ts // HALO_ROWS) - 1, 0), off + j))
    return pl.pallas_call(
        _conv_kernel,
        grid=(b, s // ts, nd),
        in_specs=[blk(0), blk(nd), blk(2 * nd), halo(nd), halo(2 * nd),
                  pl.BlockSpec((CONV_WIDTH, td), lambda bi, si, j: (0, j))],
        out_specs=blk(0),
        out_shape=jax.ShapeDtypeStruct((b, s, d), BF16),
        compiler_params=_params(("parallel", "parallel", "parallel")),
        name="short_gated_conv",
    )(proj, proj, proj, proj, proj, conv_w)


def _ffn(x, g, wi, wo, layer, slot):
    f = wo.shape[-2]
    h = rmsnorm(x, g[layer, slot], BF16)
    a = matmul_swiglu(h, wi, layer, slot)
    return matmul_residual(a, wo.astype(BF16), x, FFN_HALF, tm=512, tn=1024, tk=f // 2, w_index=(layer, slot))


def kernel(x, ffn_norm, ffn_wi, ffn_wo, mix_norm, attn_wqkv, attn_wo, conv_win, conv_w, conv_wout, final_norm):
    b, s, d = x.shape
    depth = ffn_norm.shape[0]
    xf = x.reshape(b * s, d)
    for i in range(depth):
        xf = _ffn(xf, ffn_norm, ffn_wi, ffn_wo, i, 0)
        h = rmsnorm(xf, mix_norm[i], BF16)
        j = i // 2
        if i % 2 == 0:
            qkv = matmul(h, attn_wqkv[j].astype(BF16), BF16, scaled_cols=d, col_scale=LOG2E * HEAD_DIM ** -0.5)
            o = stick_breaking_attention(qkv.reshape(b, s, 3 * d), N_HEADS, HEAD_DIM)
            xf = matmul_residual(o.reshape(b * s, d), attn_wo[j].astype(BF16), xf, 1.0, tm=1024, tn=1024, tk=d)
        else:
            proj = matmul(h, conv_win[j].astype(BF16), BF16)
            y = short_gated_conv_core(proj.reshape(b, s, 3 * d), conv_w[j])
            xf = matmul_residual(y.reshape(b * s, d), conv_wout[j].astype(BF16), xf, 1.0, tm=1024, tn=1024, tk=d)
        xf = _ffn(xf, ffn_norm, ffn_wi, ffn_wo, i, 1)
    return rmsnorm(xf, final_norm, F32).reshape(b, s, d)
```
